```python
import jax, jax.numpy as jnp
from jax import lax
import numpy as np

D_MODEL = 1024
BATCH = 8
SEQ = 2048
DEPTH = 4
DEC_BATCH = 128
DEC_SEQ = 8
PAST_LEN = 16384
PAGE_SIZE = 128

E_MIX = 2 * D_MODEL
HEAD_DIM = 128
W_A = 6 * HEAD_DIM
W_B = 4 * HEAD_DIM
W_C = E_MIX - W_A - W_B
N_HEADS_A = W_A // HEAD_DIM
CHUNK = 128
POOL_WINDOWS = (2, 4, 8, 16)
N_POOL_GROUPS = len(POOL_WINDOWS)
POOL_GW = W_B // N_POOL_GROUPS
POOL_BUF = max(POOL_WINDOWS) - 1
CONV_W = 3
CONV_BUF = CONV_W - 1
EPS = 1e-6
IN_WIDTHS = (W_A, W_A, W_A, W_B, W_B, W_C, W_C, W_C, W_C)
IN_TOTAL = sum(IN_WIDTHS)
IN_SPLITS = tuple(int(s) for s in np.cumsum(IN_WIDTHS)[:-1])

kernel_name = "hymba_style_gmlp_pool_shortconv_decode_step"


def rmsnorm(x, g):
    xf = x.astype(jnp.float32)
    y = xf * lax.rsqrt(jnp.mean(xf * xf, axis=-1, keepdims=True) + EPS) * g.astype(jnp.float32)
    return y.astype(x.dtype)


def chunk_gmlp(u, v, v_g, w_s, b_s):
    bt, L, _ = v.shape
    vn = rmsnorm(v, v_g)
    n_chunks = -(-L // CHUNK)
    pad = n_chunks * CHUNK - L
    vp = jnp.pad(vn, ((0, 0), (0, pad), (0, 0)))
    vp = vp.reshape(bt, n_chunks, CHUNK, N_HEADS_A, HEAD_DIM)
    mask = jnp.tril(jnp.ones((CHUNK, CHUNK), dtype=bool))
    w_m = jnp.where(mask[None], w_s, jnp.zeros_like(w_s))
    s = jnp.einsum('hts,bcshd->bcthd', w_m, vp) + b_s.T[None, None, :, :, None]
    s = s.reshape(bt, n_chunks * CHUNK, W_A)[:, :L]
    return u * s, vn


def pool_mixer(p, buf, start_pos, w_pg, pool_scale):
    bt, L, _ = p.shape
    pp = jnp.concatenate([buf, p], axis=1)
    cs = jnp.cumsum(pp.astype(jnp.float32), axis=1)
    cs0 = jnp.pad(cs, ((0, 0), (1, 0), (0, 0)))
    pos = start_pos + jnp.arange(L, dtype=jnp.int32)
    hi = cs0[:, POOL_BUF + 1:POOL_BUF + 1 + L]
    outs = []
    for gi, w in enumerate(POOL_WINDOWS):
        c0, c1 = gi * POOL_GW, (gi + 1) * POOL_GW
        lo = cs0[:, POOL_BUF + 1 - w:POOL_BUF + 1 - w + L, c0:c1]
        cnt = jnp.minimum(pos + 1, w).astype(jnp.float32)[None, :, None]
        mean = (hi[..., c0:c1] - lo) / cnt
        d = (mean - p[..., c0:c1].astype(jnp.float32)).astype(p.dtype)
        outs.append(jnp.einsum('bld,de->ble', d, w_pg[gi]))
    out = jnp.concatenate(outs, axis=-1) * pool_scale
    return out, pp[:, -POOL_BUF:]


def short_conv(xc, bg, cg, buf, conv_w):
    L = xc.shape[1]
    cx = cg * xc
    cp = jnp.concatenate([buf, cx], axis=1)
    y = conv_w[0] * cp[:, 0:L] + conv_w[1] * cp[:, 1:L + 1] + conv_w[2] * cp[:, 2:L + 2]
    return bg * y, cp[:, -CONV_BUF:]


def trunk_layer(x, pool_buf, conv_buf, start_pos, pre_g, w_in, v_g, w_s, b_s,
                w_pg, pool_scale, conv_w, w_out, post_g):
    h = rmsnorm(x, pre_g)
    proj = jnp.einsum('bld,de->ble', h, w_in)
    u, v, z_a, p, z_b, xc, bg, cg, z_c = jnp.split(proj, IN_SPLITS, axis=-1)
    a_out, vn = chunk_gmlp(u, v, v_g, w_s, b_s)
    b_out, new_pool = pool_mixer(p, pool_buf, start_pos, w_pg, pool_scale)
    c_out, new_conv = short_conv(xc, bg, cg, conv_buf, conv_w)
    mix = jnp.concatenate([a_out * jax.nn.silu(z_a), b_out * jax.nn.silu(z_b),
                           c_out * jax.nn.silu(z_c)], axis=-1)
    out = jnp.einsum('ble,ed->bld', mix, w_out)
    return x + rmsnorm(out, post_g), new_pool, new_conv, vn


def setup_inputs(seed: int = 0) -> dict:
    key = jax.random.key(seed)
    ks = jax.random.split(key, 16)
    f = jnp.float32
    nrm = lambda k, s: jax.random.normal(k, s, dtype=f)
    return {
        "x_prompt": nrm(ks[0], (BATCH, SEQ, D_MODEL)),
        "x_sample": nrm(ks[1], (DEC_BATCH, DEC_SEQ, D_MODEL)),
        "state_pool": nrm(ks[2], (DEPTH, DEC_BATCH, POOL_BUF, W_B)),
        "state_conv": nrm(ks[3], (DEPTH, DEC_BATCH, CONV_BUF, W_C)) * 0.5,
        "pre_norm_g": 1.0 + 0.05 * nrm(ks[4], (DEPTH, D_MODEL)),
        "w_in": nrm(ks[5], (DEPTH, D_MODEL, IN_TOTAL)) * D_MODEL ** -0.5,
        "v_norm_g": 1.0 + 0.05 * nrm(ks[6], (DEPTH, W_A)),
        "w_spatial": nrm(ks[7], (DEPTH, N_HEADS_A, CHUNK, CHUNK)) * CHUNK ** -0.5,
        "b_spatial": 1.0 + 0.1 * nrm(ks[8], (DEPTH, N_HEADS_A, CHUNK)),
        "w_pool_group": nrm(ks[9], (DEPTH, N_POOL_GROUPS, POOL_GW, POOL_GW)) * POOL_GW ** -0.5,
        "pool_scale": 1.0 + 0.1 * nrm(ks[10], (DEPTH, W_B)),
        "conv_w": nrm(ks[11], (DEPTH, CONV_W, W_C)) * CONV_W ** -0.5,
        "w_out": nrm(ks[12], (DEPTH, E_MIX, D_MODEL)) * E_MIX ** -0.5,
        "post_norm_g": 1.0 + 0.05 * nrm(ks[13], (DEPTH, D_MODEL)),
    }


def reference(x_prompt, x_sample, state_pool, state_conv, pre_norm_g, w_in, v_norm_g,
              w_spatial, b_spatial, w_pool_group, pool_scale, conv_w, w_out, post_norm_g):
    hp = x_prompt
    hs = x_sample
    pool_p, conv_p, pool_s, conv_s, v_s = [], [], [], [], []
    zero_pool = jnp.zeros((x_prompt.shape[0], POOL_BUF, W_B), x_prompt.dtype)
    zero_conv = jnp.zeros((x_prompt.shape[0], CONV_BUF, W_C), x_prompt.dtype)
    for l in range(DEPTH):
        params = (pre_norm_g[l], w_in[l], v_norm_g[l], w_spatial[l], b_spatial[l],
                  w_pool_group[l], pool_scale[l], conv_w[l], w_out[l], post_norm_g[l])
        hp, npool, nconv, _ = trunk_layer(hp, zero_pool, zero_conv, 0, *params)
        pool_p.append(npool)
        conv_p.append(nconv)
        hs, npool, nconv, vn = trunk_layer(hs, state_pool[l], state_conv[l], PAST_LEN, *params)
        pool_s.append(npool)
        conv_s.append(nconv)
        v_s.append(vn)
    return (hp, hs, jnp.stack(pool_p), jnp.stack(conv_p), jnp.stack(pool_s),
            jnp.stack(conv_s), jnp.stack(v_s))
```

```python
import functools

import jax
import jax.numpy as jnp
from jax import lax
from jax.experimental import pallas as pl
from jax.experimental.pallas import tpu as pltpu

D_MODEL = 1024
E_MIX = 2 * D_MODEL
HEAD_DIM = 128
W_A = 6 * HEAD_DIM
W_B = 4 * HEAD_DIM
W_C = E_MIX - W_A - W_B
N_HEADS_A = W_A // HEAD_DIM
CHUNK = 128
POOL_WINDOWS = (2, 4, 8, 16)
POOL_GW = W_B // len(POOL_WINDOWS)
POOL_BUF = max(POOL_WINDOWS) - 1
CONV_W = 3
CONV_BUF = CONV_W - 1
EPS = 1e-6
IN_TOTAL = 3 * W_A + 2 * W_B + 4 * W_C

OFF_A = 0
OFF_B = 3 * W_A
OFF_C = OFF_B + 2 * W_B

SUBLANES = 8
POOL_HALO = 16
CONV_HALO = 8

PROMPT_TILE = 256
SAMPLE_TILE = 256
VMEM_LIMIT_BYTES = 56 * 1024 * 1024


def _rms(x, g):
    ms = jnp.mean(x * x, axis=-1, keepdims=True)
    return x * lax.rsqrt(ms + EPS) * g


def _silu(z):
    return z * jax.nn.sigmoid(z)


def _dot(a, b):
    return jnp.dot(a, b, preferred_element_type=jnp.float32)


def _shift_rows(prev, cur, k):
    rows = lax.broadcasted_iota(jnp.int32, cur.shape, 1)
    return jnp.where(rows < k, pltpu.roll(prev, k, 1), pltpu.roll(cur, k, 1))


def _spatial_mask(block_rows):
    r = lax.broadcasted_iota(jnp.int32, (CHUNK, CHUNK), 0)
    c = lax.broadcasted_iota(jnp.int32, (CHUNK, CHUNK), 1)
    m = c <= r
    if block_rows < CHUNK:
        m = m & ((r // block_rows) == (c // block_rows))
    return m


def _branch_a(h, w_in_ref, v_g, w_sp_ref, b_sp, mix_ref, block_rows):
    tm = h.shape[0]
    uvz = _dot(h, w_in_ref[:, OFF_A:OFF_A + 3 * W_A])
    u = uvz[:, 0:W_A]
    v = uvz[:, W_A:2 * W_A]
    za = uvz[:, 2 * W_A:3 * W_A]
    vn = _rms(v, v_g)
    vnb = vn.astype(jnp.bfloat16)
    mask = _spatial_mask(block_rows)
    wm = [jnp.where(mask, w_sp_ref[hh], 0.0).astype(jnp.bfloat16) for hh in range(N_HEADS_A)]
    rows = []
    for c in range(tm // CHUNK):
        heads = [
            _dot(wm[hh], vnb[c * CHUNK:(c + 1) * CHUNK, hh * HEAD_DIM:(hh + 1) * HEAD_DIM])
            for hh in range(N_HEADS_A)
        ]
        rows.append(jnp.concatenate(heads, axis=1) + b_sp)
    s = jnp.concatenate(rows, axis=0)
    mix_ref[:, 0:W_A] = (u * s * _silu(za)).astype(jnp.bfloat16)
    return vn


def _pool_tail(d_groups, zb, w_pg_ref, pool_scale, mix_ref):
    outs = [
        _dot(d_groups[g].astype(jnp.bfloat16), w_pg_ref[g])
        for g in range(len(POOL_WINDOWS))
    ]
    b_out = jnp.concatenate(outs, axis=1) * pool_scale
    mix_ref[:, W_A:W_A + W_B] = (b_out * _silu(zb)).astype(jnp.bfloat16)


def _finish(x, mix_ref, w_out_ref, post_g, y_ref):
    out = _dot(mix_ref[...], w_out_ref[...])
    y_ref[...] = x + _rms(out, post_g)


def _prompt_kernel(x_ref, pre_g_ref, w_in_ref, v_g_ref, w_sp_ref, b_sp_ref, w_pg_ref,
                   pool_scale_ref, conv_w_ref, w_out_ref, post_g_ref,
                   y_ref, pool_out_ref, conv_out_ref,
                   mix_ref, pool_carry_ref, conv_carry_ref, *, tiles_per_seq):
    tm = x_ref.shape[0]
    tile_in_seq = pl.program_id(0) % tiles_per_seq

    @pl.when(tile_in_seq == 0)
    def _():
        pool_carry_ref[...] = jnp.zeros_like(pool_carry_ref)
        conv_carry_ref[...] = jnp.zeros_like(conv_carry_ref)

    x = x_ref[...]
    h = _rms(x, pre_g_ref[...]).astype(jnp.bfloat16)

    _branch_a(h, w_in_ref, v_g_ref[...], w_sp_ref, b_sp_ref[...], mix_ref, CHUNK)

    pz = _dot(h, w_in_ref[:, OFF_B:OFF_B + 2 * W_B])
    p = pz[:, 0:W_B]
    zb = pz[:, W_B:2 * W_B]
    pp = jnp.concatenate([pool_carry_ref[...], p], axis=0)
    pos = tile_in_seq * tm + lax.broadcasted_iota(jnp.int32, (tm, POOL_GW), 0)
    d_groups = []
    for g, w in enumerate(POOL_WINDOWS):
        s = pp[:, g * POOL_GW:(g + 1) * POOL_GW]
        step = 1
        while step < w:
            s = s + pltpu.roll(s, step, 0)
            step *= 2
        cnt = jnp.minimum(pos + 1, w).astype(jnp.float32)
        mean = s[POOL_HALO:] / cnt
        d_groups.append(mean - p[:, g * POOL_GW:(g + 1) * POOL_GW])
    _pool_tail(d_groups, zb, w_pg_ref, pool_scale_ref[...], mix_ref)
    pool_carry_ref[...] = pp[tm:tm + POOL_HALO]
    pool_out_ref[0] = pool_carry_ref[...]

    xbcz = _dot(h, w_in_ref[:, OFF_C:OFF_C + 4 * W_C])
    xc = xbcz[:, 0:W_C]
    bg = xbcz[:, W_C:2 * W_C]
    cg = xbcz[:, 2 * W_C:3 * W_C]
    zc = xbcz[:, 3 * W_C:4 * W_C]
    cx = cg * xc
    cxp = jnp.concatenate([conv_carry_ref[...], cx], axis=0)
    cw = conv_w_ref[...]
    yc = (cw[0:1] * pltpu.roll(cxp, 2, 0) + cw[1:2] * pltpu.roll(cxp, 1, 0)
          + cw[2:3] * cxp)[CONV_HALO:]
    mix_ref[:, W_A + W_B:E_MIX] = (bg * yc * _silu(zc)).astype(jnp.bfloat16)
    conv_carry_ref[...] = cxp[tm:tm + CONV_HALO]
    conv_out_ref[0] = conv_carry_ref[...]

    _finish(x, mix_ref, w_out_ref, post_g_ref[...], y_ref)


def _sample_kernel(x_ref, pool_hist_ref, conv_hist_ref, pre_g_ref, w_in_ref, v_g_ref,
                   w_sp_ref, b_sp_ref, w_pg_ref, pool_scale_ref, conv_w_ref, w_out_ref,
                   post_g_ref,
                   y_ref, pool_out_ref, conv_out_ref, vn_ref,
                   mix_ref, *, seq_len):
    tm = x_ref.shape[0]
    nseq = tm // seq_len
    x = x_ref[...]
    h = _rms(x, pre_g_ref[...]).astype(jnp.bfloat16)

    vn_ref[...] = _branch_a(h, w_in_ref, v_g_ref[...], w_sp_ref, b_sp_ref[...], mix_ref, seq_len)

    pz = _dot(h, w_in_ref[:, OFF_B:OFF_B + 2 * W_B])
    p = pz[:, 0:W_B]
    zb = pz[:, W_B:2 * W_B]
    p3 = p.reshape(nseq, seq_len, W_B)
    hist = pool_hist_ref[...]
    d_groups = []
    for g, w in enumerate(POOL_WINDOWS):
        cols = slice(g * POOL_GW, (g + 1) * POOL_GW)
        tiles = [hist[:, 0:SUBLANES, cols], hist[:, SUBLANES:2 * SUBLANES, cols], p3[:, :, cols]]
        step = 1
        while step < w:
            if step < SUBLANES:
                tiles = [tiles[0] + pltpu.roll(tiles[0], step, 1)] + [
                    tiles[j] + _shift_rows(tiles[j - 1], tiles[j], step)
                    for j in range(1, len(tiles))]
            else:
                tiles = [tiles[0]] + [tiles[j] + tiles[j - 1] for j in range(1, len(tiles))]
            step *= 2
        mean = tiles[2] * (1.0 / w)
        d_groups.append((mean - p3[:, :, cols]).reshape(tm, POOL_GW))
    _pool_tail(d_groups, zb, w_pg_ref, pool_scale_ref[...], mix_ref)
    pool_out_ref[:, 0:SUBLANES, :] = hist[:, SUBLANES:2 * SUBLANES, :]
    pool_out_ref[:, SUBLANES:2 * SUBLANES, :] = p3

    xbcz = _dot(h, w_in_ref[:, OFF_C:OFF_C + 4 * W_C])
    xc = xbcz[:, 0:W_C]
    bg = xbcz[:, W_C:2 * W_C]
    cg = xbcz[:, 2 * W_C:3 * W_C]
    zc = xbcz[:, 3 * W_C:4 * W_C]
    cx3 = (cg * xc).reshape(nseq, seq_len, W_C)
    chist = conv_hist_ref[...]
    cw = conv_w_ref[...]
    yc = (cw[0:1] * _shift_rows(chist, cx3, 2) + cw[1:2] * _shift_rows(chist, cx3, 1)
          + cw[2:3] * cx3).reshape(tm, W_C)
    mix_ref[:, W_A + W_B:E_MIX] = (bg * yc * _silu(zc)).astype(jnp.bfloat16)
    conv_out_ref[...] = cx3

    _finish(x, mix_ref, w_out_ref, post_g_ref[...], y_ref)


def _const_spec(shape):
    zeros = (0,) * len(shape)
    return pl.BlockSpec(shape, lambda i: zeros, pipeline_mode=pl.Buffered(1))


def _weight_specs():
    return [
        _const_spec((D_MODEL, IN_TOTAL)),
        _const_spec((1, W_A)),
        _const_spec((N_HEADS_A, CHUNK, CHUNK)),
        _const_spec((CHUNK, W_A)),
        _const_spec((len(POOL_WINDOWS), POOL_GW, POOL_GW)),
        _const_spec((1, W_B)),
        _const_spec((CONV_W, W_C)),
        _const_spec((E_MIX, D_MODEL)),
        _const_spec((1, D_MODEL)),
    ]


def _prompt_layer(x2, batch, seq, pre_g, w_in, v_g, w_sp, b_sp, w_pg, pool_scale, conv_w,
                  w_out, post_g):
    tm = PROMPT_TILE
    tiles_per_seq = seq // tm
    n_tiles = batch * tiles_per_seq
    return pl.pallas_call(
        functools.partial(_prompt_kernel, tiles_per_seq=tiles_per_seq),
        grid=(n_tiles,),
        in_specs=[pl.BlockSpec((tm, D_MODEL), lambda i: (i, 0)),
                  _const_spec((1, D_MODEL))] + _weight_specs(),
        out_specs=[
            pl.BlockSpec((tm, D_MODEL), lambda i: (i, 0)),
            pl.BlockSpec((1, POOL_HALO, W_B), lambda i: (i // tiles_per_seq, 0, 0)),
            pl.BlockSpec((1, CONV_HALO, W_C), lambda i: (i // tiles_per_seq, 0, 0)),
        ],
        out_shape=[
            jax.ShapeDtypeStruct((batch * seq, D_MODEL), jnp.float32),
            jax.ShapeDtypeStruct((batch, POOL_HALO, W_B), jnp.float32),
            jax.ShapeDtypeStruct((batch, CONV_HALO, W_C), jnp.float32),
        ],
        scratch_shapes=[
            pltpu.VMEM((tm, E_MIX), jnp.bfloat16),
            pltpu.VMEM((POOL_HALO, W_B), jnp.float32),
            pltpu.VMEM((CONV_HALO, W_C), jnp.float32),
        ],
        compiler_params=pltpu.CompilerParams(
            dimension_semantics=("arbitrary",), vmem_limit_bytes=VMEM_LIMIT_BYTES),
        name="prompt_layer",
    )(x2, pre_g, w_in, v_g, w_sp, b_sp, w_pg, pool_scale, conv_w, w_out, post_g)


def _sample_layer(x2, pool_hist, conv_hist, seq_len, pre_g, w_in, v_g, w_sp, b_sp, w_pg,
                  pool_scale, conv_w, w_out, post_g):
    tm = SAMPLE_TILE
    n_tok = x2.shape[0]
    nseq = tm // seq_len
    n_seq_total = n_tok // seq_len
    return pl.pallas_call(
        functools.partial(_sample_kernel, seq_len=seq_len),
        grid=(n_tok // tm,),
        in_specs=[pl.BlockSpec((tm, D_MODEL), lambda i: (i, 0)),
                  pl.BlockSpec((nseq, POOL_HALO, W_B), lambda i: (i, 0, 0)),
                  pl.BlockSpec((nseq, CONV_HALO, W_C), lambda i: (i, 0, 0)),
                  _const_spec((1, D_MODEL))] + _weight_specs(),
        out_specs=[
            pl.BlockSpec((tm, D_MODEL), lambda i: (i, 0)),
            pl.BlockSpec((nseq, POOL_HALO, W_B), lambda i: (i, 0, 0)),
            pl.BlockSpec((nseq, seq_len, W_C), lambda i: (i, 0, 0)),
            pl.BlockSpec((tm, W_A), lambda i: (i, 0)),
        ],
        out_shape=[
            jax.ShapeDtypeStruct((n_tok, D_MODEL), jnp.float32),
            jax.ShapeDtypeStruct((n_seq_total, POOL_HALO, W_B), jnp.float32),
            jax.ShapeDtypeStruct((n_seq_total, seq_len, W_C), jnp.float32),
            jax.ShapeDtypeStruct((n_tok, W_A), jnp.float32),
        ],
        scratch_shapes=[pltpu.VMEM((tm, E_MIX), jnp.bfloat16)],
        compiler_params=pltpu.CompilerParams(
            dimension_semantics=("arbitrary",), vmem_limit_bytes=VMEM_LIMIT_BYTES),
        name="sample_layer",
    )(x2, pool_hist, conv_hist, pre_g, w_in, v_g, w_sp, b_sp, w_pg, pool_scale, conv_w,
      w_out, post_g)


def kernel(x_prompt, x_sample, state_pool, state_conv, pre_norm_g, w_in, v_norm_g, w_spatial,
           b_spatial, w_pool_group, pool_scale, conv_w, w_out, post_norm_g):
    batch, seq, _ = x_prompt.shape
    dec_batch, dec_seq, _ = x_sample.shape
    depth = w_in.shape[0]
    assert seq % PROMPT_TILE == 0 and PROMPT_TILE % CHUNK == 0
    assert dec_seq == SUBLANES and (dec_batch * dec_seq) % SAMPLE_TILE == 0

    w_in_b = w_in.astype(jnp.bfloat16)
    w_out_b = w_out.astype(jnp.bfloat16)
    w_pg_b = w_pool_group.astype(jnp.bfloat16)
    b_sp_prompt = jnp.repeat(jnp.swapaxes(b_spatial, 1, 2), HEAD_DIM, axis=2)
    reps = CHUNK // dec_seq
    w_sp_sample = jnp.tile(w_spatial[:, :, :dec_seq, :dec_seq], (1, 1, reps, reps))
    b_sp_sample = jnp.tile(b_sp_prompt[:, :dec_seq, :], (1, reps, 1))
    pool_hist = jnp.pad(state_pool, ((0, 0), (0, 0), (POOL_HALO - POOL_BUF, 0), (0, 0)))
    conv_hist = jnp.pad(state_conv, ((0, 0), (0, 0), (CONV_HALO - CONV_BUF, 0), (0, 0)))

    hp = x_prompt.reshape(batch * seq, D_MODEL)
    hs = x_sample.reshape(dec_batch * dec_seq, D_MODEL)
    pool_p, conv_p, pool_s, conv_s, v_s = [], [], [], [], []
    for l in range(depth):
        shared = (w_in_b[l], v_norm_g[l][None], None, None, w_pg_b[l], pool_scale[l][None],
                  conv_w[l], w_out_b[l], post_norm_g[l][None])

        def with_spatial(w_sp, b_sp):
            return shared[:2] + (w_sp, b_sp) + shared[4:]

        hp, npool, nconv = _prompt_layer(hp, batch, seq, pre_norm_g[l][None],
                                         *with_spatial(w_spatial[l], b_sp_prompt[l]))
        pool_p.append(npool[:, POOL_HALO - POOL_BUF:])
        conv_p.append(nconv[:, CONV_HALO - CONV_BUF:])
        hs, npool, nconv, vn = _sample_layer(hs, pool_hist[l], conv_hist[l], dec_seq,
                                             pre_norm_g[l][None],
                                             *with_spatial(w_sp_sample[l], b_sp_sample[l]))
        pool_s.append(npool[:, POOL_HALO - POOL_BUF:])
        conv_s.append(nconv[:, dec_seq - CONV_BUF:])
        v_s.append(vn.reshape(dec_batch, dec_seq, W_A))
    return (hp.reshape(batch, seq, D_MODEL), hs.reshape(dec_batch, dec_seq, D_MODEL),
            jnp.stack(pool_p), jnp.stack(conv_p), jnp.stack(pool_s), jnp.stack(conv_s),
            jnp.stack(v_s))
```
